```python
import jax, jax.numpy as jnp
from jax import lax
import numpy as np

D_MODEL = 1024
BATCH = 16
SEQ = 4096
DEPTH = 4

GRID_W = 64
CTX_LEN = 256
CONV_W = 512
CONV_K = 31
CONV_GROUPS = 8
RWKV_W = D_MODEL - CONV_W
HEAD = 64
N_HEADS = RWKV_W // HEAD
DECAY_RANK = 64
ICLR_RANK = 64
GATE_RANK = 160
SHIFT_K = 3
CONV_IN = 2 * CONV_W
RWKV_IN = 3 * RWKV_W + DECAY_RANK + ICLR_RANK + GATE_RANK
IN_COLS = CONV_IN + RWKV_IN
D_FF = -(-8 * D_MODEL // (3 * 256)) * 256
RMS_EPS = 1e-6
LN_EPS = 1e-5
GN_EPS = 64e-5
DECAY_SCALE = 0.606531

kernel_name = "hybrid_conformer_rwkv7_dit"


def rmsnorm(x, g):
    xf = x.astype(jnp.float32)
    y = xf * lax.rsqrt(jnp.mean(xf * xf, axis=-1, keepdims=True) + RMS_EPS)
    return (y * g).astype(x.dtype)


def modulate(h, shift, scale):
    return h * (1.0 + scale) + shift


def group_norm(x, n_groups, g, b, eps):
    shp = x.shape
    xg = x.astype(jnp.float32).reshape(shp[:-1] + (n_groups, shp[-1] // n_groups))
    mu = jnp.mean(xg, axis=-1, keepdims=True)
    var = jnp.mean(jnp.square(xg - mu), axis=-1, keepdims=True)
    xg = (xg - mu) * lax.rsqrt(var + eps)
    return xg.reshape(shp) * g + b


def dwconv(x, w):
    ch = x.shape[-1]
    return lax.conv_general_dilated(
        x, w[:, None, :].astype(x.dtype), window_strides=(1,), padding="SAME",
        dimension_numbers=("NWC", "WIO", "NWC"), feature_group_count=ch)


def grid_transpose(t, rows, cols):
    b, _, ch = t.shape
    return t.reshape(b, rows, cols, ch).transpose(0, 2, 1, 3).reshape(b, rows * cols, ch)


def to_heads(t):
    return t.reshape(t.shape[:-1] + (N_HEADS, HEAD))


def conformer_conv(u, conv_w, conv_b, cnorm_g, cnorm_b):
    val, gate = jnp.split(u, 2, axis=-1)
    z = val * jax.nn.sigmoid(gate)
    z = dwconv(z, conv_w) + conv_b
    z = group_norm(z, CONV_GROUPS, cnorm_g, cnorm_b, LN_EPS)
    return jax.nn.silu(z)


def rwkv_prep(u, shift_w, decay_b0, decay_up, iclr_b0, iclr_up, k_k, k_a, g_up):
    u = dwconv(u, shift_w).astype(jnp.float32)
    o1 = 3 * RWKV_W
    o2 = o1 + DECAY_RANK
    o3 = o2 + ICLR_RANK
    r, k, v, w_lo, a_lo, g_lo = jnp.split(u, [RWKV_W, 2 * RWKV_W, o1, o2, o3], axis=-1)
    kk = to_heads(k * k_k)
    kk = kk / jnp.maximum(jnp.sqrt(jnp.sum(kk * kk, axis=-1, keepdims=True)), 1e-12)
    dirs = []
    for d in range(2):
        decay = jnp.exp(-DECAY_SCALE * jax.nn.sigmoid(decay_b0[d] + jnp.tanh(w_lo) @ decay_up[d]))
        a = jax.nn.sigmoid(iclr_b0[d] + a_lo @ iclr_up[d])
        k_d = k * (1.0 + (a - 1.0) * k_a)
        dirs.append((to_heads(k_d), to_heads(decay), to_heads(a)))
    g = jax.nn.sigmoid(g_lo) @ g_up
    return to_heads(r), to_heads(k), to_heads(v), kk, dirs, g


def wkv_scan(r, k, v, decay, kk, a, s0, reverse, emit):
    xs = tuple(jnp.swapaxes(t, 0, 1) for t in (r, k, v, decay, kk, kk * a))

    def step(s, inp):
        r_t, k_t, v_t, w_t, kk_t, b_t = inp
        sa = -jnp.einsum("bhvk,bhk->bhv", s, kk_t)
        s = (s * w_t[:, :, None, :] + sa[..., None] * b_t[:, :, None, :]
             + v_t[..., None] * k_t[:, :, None, :])
        y = jnp.einsum("bhvk,bhk->bhv", s, r_t) if emit else None
        return s, y

    s_fin, ys = lax.scan(step, s0, xs, reverse=reverse)
    return (jnp.swapaxes(ys, 0, 1) if emit else None), s_fin


def rwkv_scans(prep, s0_f, s0_b, emit):
    r, k, v, kk, dirs, g = prep
    (k_f, w_f, a_f), (k_b, w_b, a_b) = dirs
    y_f, s_f = wkv_scan(r, k_f, v, w_f, kk, a_f, s0_f, False, emit)
    y_b, s_b = wkv_scan(r, k_b, v, w_b, kk, a_b, s0_b, True, emit)
    y = (y_f + y_b) if emit else None
    return y, s_f, s_b


def rwkv_out(y, prep, r_k, gn_g, gn_b):
    r, k, v, kk, dirs, g = prep
    b, t = y.shape[:2]
    y = group_norm(y.reshape(b, t, RWKV_W), N_HEADS, gn_g, gn_b, GN_EPS)
    bonus = jnp.sum(r * k * to_heads(r_k), axis=-1, keepdims=True) * v
    return (y + bonus.reshape(b, t, RWKV_W)) * g


def swiglu(h, wg, wu, wd):
    return (jax.nn.silu(h @ wg) * (h @ wu)) @ wd


def setup_inputs(seed: int = 0) -> dict:
    key = jax.random.key(seed)
    ks = iter(jax.random.split(key, 40))

    def nrm(shape, scale):
        return jax.random.normal(next(ks), shape, jnp.float32) * scale

    L = DEPTH
    shift_base = jnp.array([0.25, 1.0, 0.25], jnp.float32)[None, :, None]
    return {
        "x": nrm((BATCH, SEQ, D_MODEL), 1.0),
        "c": nrm((BATCH, D_MODEL), 1.0),
        "ctx": nrm((BATCH, CTX_LEN, D_MODEL), 1.0),
        "c_ctx": nrm((D_MODEL,), 1.0),
        "ada_w": nrm((L, D_MODEL, 6 * D_MODEL), 0.5 * D_MODEL ** -0.5),
        "ada_b": nrm((L, 6 * D_MODEL), 0.02),
        "norm1_g": 1.0 + nrm((L, D_MODEL), 0.02),
        "norm2_g": 1.0 + nrm((L, D_MODEL), 0.02),
        "w_in": nrm((L, D_MODEL, IN_COLS), D_MODEL ** -0.5),
        "conv_w": nrm((L, CONV_K, CONV_W), CONV_K ** -0.5),
        "conv_b": nrm((L, CONV_W), 0.02),
        "cnorm_g": 1.0 + nrm((L, CONV_W), 0.02),
        "cnorm_b": nrm((L, CONV_W), 0.02),
        "shift_w": shift_base + nrm((L, SHIFT_K, RWKV_IN), 0.1),
        "decay_b0": jax.random.uniform(next(ks), (L, 2, RWKV_W), jnp.float32, -6.0, 0.0),
        "decay_up": nrm((L, 2, DECAY_RANK, RWKV_W), 0.5 * DECAY_RANK ** -0.5),
        "iclr_b0": nrm((L, 2, RWKV_W), 0.5),
        "iclr_up": nrm((L, 2, ICLR_RANK, RWKV_W), 0.5 * ICLR_RANK ** -0.5),
        "k_k": 0.85 + nrm((L, RWKV_W), 0.05),
        "k_a": 1.0 + nrm((L, RWKV_W), 0.05),
        "g_up": nrm((L, GATE_RANK, RWKV_W), GATE_RANK ** -0.5),
        "r_k": nrm((L, RWKV_W), 0.1),
        "gn_g": 1.0 + nrm((L, RWKV_W), 0.02),
        "gn_b": nrm((L, RWKV_W), 0.02),
        "w_out": nrm((L, D_MODEL, D_MODEL), D_MODEL ** -0.5),
        "ffn_wg": nrm((L, D_MODEL, D_FF), D_MODEL ** -0.5),
        "ffn_wu": nrm((L, D_MODEL, D_FF), D_MODEL ** -0.5),
        "ffn_wd": nrm((L, D_FF, D_MODEL), D_FF ** -0.5),
        "final_g": 1.0 + nrm((D_MODEL,), 0.02),
    }


def reference(x, c, ctx, c_ctx, ada_w, ada_b, norm1_g, norm2_g, w_in, conv_w, conv_b,
              cnorm_g, cnorm_b, shift_w, decay_b0, decay_up, iclr_b0, iclr_up, k_k, k_a,
              g_up, r_k, gn_g, gn_b, w_out, ffn_wg, ffn_wu, ffn_wd, final_g):
    rows = x.shape[1] // GRID_W
    batch = x.shape[0]
    s_zero = jnp.zeros((batch, N_HEADS, HEAD, HEAD), jnp.float32)
    silu_c = jax.nn.silu(c)
    silu_cc = jax.nn.silu(c_ctx)
    for l in range(DEPTH):
        last = l == DEPTH - 1
        col_major = l % 2 == 1
        mod_x = (silu_c @ ada_w[l] + ada_b[l])[:, None, :]
        sh1, sc1, gt1, sh2, sc2, gt2 = jnp.split(mod_x, 6, axis=-1)
        mod_c = silu_cc @ ada_w[l] + ada_b[l]
        csh1, csc1, cgt1, csh2, csc2, cgt2 = jnp.split(mod_c, 6, axis=-1)
        rw = (shift_w[l], decay_b0[l], decay_up[l], iclr_b0[l], iclr_up[l], k_k[l], k_a[l], g_up[l])
        cv = (conv_w[l], conv_b[l], cnorm_g[l], cnorm_b[l])
        post = (r_k[l], gn_g[l], gn_b[l])

        ux = modulate(rmsnorm(x, norm1_g[l]), sh1, sc1) @ w_in[l]
        uc = modulate(rmsnorm(ctx, norm1_g[l]), csh1, csc1) @ w_in[l]
        if col_major:
            ux = grid_transpose(ux, rows, GRID_W)

        prep_c = rwkv_prep(uc[..., CONV_IN:], *rw)
        yc, s_f, s_b = rwkv_scans(prep_c, s_zero, s_zero, emit=not last)
        prep_x = rwkv_prep(ux[..., CONV_IN:], *rw)
        yx, _, _ = rwkv_scans(prep_x, s_f, s_b, emit=True)

        mix_x = jnp.concatenate(
            [conformer_conv(ux[..., :CONV_IN], *cv), rwkv_out(yx, prep_x, *post)],
            axis=-1).astype(x.dtype)
        if col_major:
            mix_x = grid_transpose(mix_x, GRID_W, rows)
        x = x + gt1 * (mix_x @ w_out[l])

        x = x + gt2 * swiglu(modulate(rmsnorm(x, norm2_g[l]), sh2, sc2),
                             ffn_wg[l], ffn_wu[l], ffn_wd[l])

        if not last:
            mix_c = jnp.concatenate(
                [conformer_conv(uc[..., :CONV_IN], *cv), rwkv_out(yc, prep_c, *post)],
                axis=-1).astype(ctx.dtype)
            ctx = ctx + cgt1 * (mix_c @ w_out[l])
            ctx = ctx + cgt2 * swiglu(modulate(rmsnorm(ctx, norm2_g[l]), csh2, csc2),
                                      ffn_wg[l], ffn_wu[l], ffn_wd[l])
    return rmsnorm(x, final_g)
```

```python
import functools

import jax
import jax.numpy as jnp
from jax import lax
from jax.experimental import pallas as pl
from jax.experimental.pallas import tpu as pltpu

F32 = jnp.float32
BF16 = jnp.bfloat16

D_MODEL = 1024
CONV_W = 512
CONV_K = 31
RWKV_W = 512
HEAD = 64
DECAY_RANK = 64
ICLR_RANK = 64
GATE_RANK = 160
RWKV_COLS = 3 * RWKV_W + DECAY_RANK + ICLR_RANK + GATE_RANK
RWKV_PAD = 1920
LORA_OFF = 3 * RWKV_W
GATE_OFF = LORA_OFF + 128
GATE_PAD = RWKV_PAD - GATE_OFF
D_FF = 2816
FF_CHUNK = 256
GRID_W = 64
RMS_EPS = 1e-6
LN_EPS = 1e-5
GN_EPS = 64e-5
DECAY_SCALE = 0.606531
CHUNK = 64
PAIR = 2 * HEAD
V7X_VMEM_LIMIT = 56 * 1024 * 1024


def _cparams(sem, vmem_bytes=None):
    return pltpu.CompilerParams(dimension_semantics=sem, vmem_limit_bytes=vmem_bytes)


def _dot(a, b):
    return jnp.dot(a, b, preferred_element_type=F32)


def _dot_nt(a, b):
    return lax.dot_general(a, b, (((1,), (1,)), ((), ())), preferred_element_type=F32)


def _dot_tn(a, b):
    return lax.dot_general(a, b, (((0,), (0,)), ((), ())), preferred_element_type=F32)


def _sigmoid(x):
    return 1.0 / (1.0 + jnp.exp(-x))


def _dot2(x, m):
    hi = x.astype(BF16)
    lo = (x - hi.astype(F32)).astype(BF16)
    return _dot(hi, m) + _dot(lo, m)


def _rms_mod(x, g, shift, scale):
    y = x * lax.rsqrt(jnp.mean(x * x, axis=-1, keepdims=True) + RMS_EPS)
    return (y * g) * (1.0 + scale) + shift


def _ada_kernel(cc_ref, w_ref, b_ref, o_ref):
    s = cc_ref[...]
    a = (s * _sigmoid(s)).astype(BF16)
    o_ref[0] = _dot(a, w_ref[0].astype(BF16)) + b_ref[0]


def _ada(cc, ada_w, ada_b):
    L, D, N = ada_w.shape
    R = cc.shape[0]
    tn = 1536
    return pl.pallas_call(
        _ada_kernel,
        grid=(L, N // tn),
        in_specs=[pl.BlockSpec((R, D), lambda l, j: (0, 0)),
                  pl.BlockSpec((1, D, tn), lambda l, j: (l, 0, j)),
                  pl.BlockSpec((1, 1, tn), lambda l, j: (l, 0, j))],
        out_specs=pl.BlockSpec((1, R, tn), lambda l, j: (l, 0, j)),
        out_shape=jax.ShapeDtypeStruct((L, R, N), F32),
        compiler_params=_cparams(("arbitrary", "arbitrary"), 40 * 1024 * 1024),
        name="ada",
    )(cc, ada_w, ada_b.reshape(L, 1, N))


def _mod_spec(m):
    if m.shape[0] == 1:
        return pl.BlockSpec((1, 1, D_MODEL), lambda b, i: (0, 0, 0))
    return pl.BlockSpec((1, 1, D_MODEL), lambda b, i: (b, 0, 0))


def _row_spec(n):
    return pl.BlockSpec((1, n), lambda b, i: (0, 0))


def _full_spec(shape):
    nd = len(shape)
    return pl.BlockSpec(shape, lambda b, i: (0,) * nd)


def _proj_in_kernel(x_ref, g_ref, sh_ref, sc_ref, wc_ref, wr_ref, uc_ref, ur_ref):
    h = _rms_mod(x_ref[0], g_ref[...], sh_ref[0], sc_ref[0]).astype(BF16)
    uc_ref[0] = _dot(h, wc_ref[...]).astype(BF16)
    ur_ref[0] = _dot(h, wr_ref[...])


def _proj_in(x, g, sh, sc, wc, wr, tm):
    B, T, D = x.shape
    return pl.pallas_call(
        _proj_in_kernel,
        grid=(B, T // tm),
        in_specs=[pl.BlockSpec((1, tm, D), lambda b, i: (b, i, 0)),
                  _row_spec(D), _mod_spec(sh), _mod_spec(sc),
                  _full_spec(wc.shape), _full_spec(wr.shape)],
        out_specs=[pl.BlockSpec((1, tm, 2 * CONV_W), lambda b, i: (b, i, 0)),
                   pl.BlockSpec((1, tm, RWKV_PAD), lambda b, i: (b, i, 0))],
        out_shape=[jax.ShapeDtypeStruct((B, T, 2 * CONV_W), BF16),
                   jax.ShapeDtypeStruct((B, T, RWKV_PAD), F32)],
        compiler_params=_cparams(("parallel", "parallel"), 48 * 1024 * 1024),
        name="proj_in",
    )(x, g, sh, sc, wc, wr)


def _rwkv_prep_kernel(cur_ref, prev_ref, next_ref, shw_ref, db0_ref, dup_ref, ib0_ref, iup_ref,
                      kk_ref, ka_ref, gup_ref, rk_ref, bd_ref, tri_ref,
                      q_ref, p_ref, k_ref, r_ref, gc_ref, v_ref, bv_ref, g_ref, us_ref, *, tm):
    i = pl.program_id(1)
    nt = pl.num_programs(1)
    us_ref[0:8, :] = jnp.where(i > 0, prev_ref[0], 0.0)
    us_ref[8:8 + tm, :] = cur_ref[0]
    us_ref[8 + tm:16 + tm, :] = jnp.where(i < nt - 1, next_ref[0], 0.0)

    def shifted(c0, c1):
        return (shw_ref[0:1, c0:c1] * us_ref[7:7 + tm, c0:c1]
                + shw_ref[1:2, c0:c1] * us_ref[8:8 + tm, c0:c1]
                + shw_ref[2:3, c0:c1] * us_ref[9:9 + tm, c0:c1])

    r = shifted(0, RWKV_W)
    k = shifted(RWKV_W, 2 * RWKV_W)
    v = shifted(2 * RWKV_W, 3 * RWKV_W)
    wa = shifted(LORA_OFF, LORA_OFF + 128)
    gl = shifted(GATE_OFF, RWKV_PAD)
    bd = bd_ref[...]

    kkr = k * kk_ref[...]
    kk = kkr / jnp.maximum(jnp.sqrt(_dot2(kkr * kkr, bd)), 1e-12)
    lane = lax.broadcasted_iota(jnp.int32, (tm, 128), 1)
    wa_t = jnp.where(lane < DECAY_RANK, jnp.tanh(wa), wa).astype(BF16)

    v_ref[0] = v.astype(BF16)
    g_ref[0] = _dot(_sigmoid(gl).astype(BF16), gup_ref[...]).astype(BF16)
    bv_ref[0] = (_dot2(r * k * rk_ref[...], bd) * v).astype(BF16)

    for d in range(2):
        lw = -DECAY_SCALE * _sigmoid(db0_ref[d] + _dot(wa_t, dup_ref[d]))
        a = _sigmoid(ib0_ref[d] + _dot(wa_t, iup_ref[d]))
        kd = k * (1.0 + (a - 1.0) * ka_ref[...])
        b = kk * a
        tri = tri_ref[d]
        for c in range(tm // CHUNK):
            sl = slice(c * CHUNK, (c + 1) * CHUNK)
            lwc = lw[sl]
            cum = _dot2_lhs(tri, lwc)
            tot = cum[0:1] if d == 1 else cum[CHUNK - 1:CHUNK]
            e_neg = jnp.exp(-cum)
            q_ref[d, 0, sl, :] = (-kk[sl] * jnp.exp(cum - lwc)).astype(BF16)
            p_ref[d, 0, sl, :] = (b[sl] * e_neg).astype(BF16)
            k_ref[d, 0, sl, :] = (kd[sl] * e_neg).astype(BF16)
            r_ref[d, 0, sl, :] = (r[sl] * jnp.exp(cum)).astype(BF16)
            gc_ref[d, 0, c] = jnp.exp(tot)


def _dot2_lhs(m, x):
    hi = x.astype(BF16)
    lo = (x - hi.astype(F32)).astype(BF16)
    return _dot(m, hi) + _dot(m, lo)


def _rwkv_prep(ur, pp, tm):
    B, T, _ = ur.shape
    nh8 = tm // 8
    nc = tm // CHUNK
    tok = lambda dt: jax.ShapeDtypeStruct((2, B, T, RWKV_W), dt)
    tok_spec = pl.BlockSpec((2, 1, tm, RWKV_W), lambda b, i: (0, b, i, 0))
    one = lambda dt: jax.ShapeDtypeStruct((B, T, RWKV_W), dt)
    one_spec = pl.BlockSpec((1, tm, RWKV_W), lambda b, i: (b, i, 0))
    params = (pp["shw"], pp["db0"], pp["dup"], pp["ib0"], pp["iup"], pp["kk"], pp["ka"],
              pp["gup"], pp["rk"], pp["bd"], pp["tri"])
    return pl.pallas_call(
        functools.partial(_rwkv_prep_kernel, tm=tm),
        grid=(B, T // tm),
        in_specs=[pl.BlockSpec((1, tm, RWKV_PAD), lambda b, i: (b, i, 0)),
                  pl.BlockSpec((1, 8, RWKV_PAD), lambda b, i: (b, jnp.maximum(i * nh8 - 1, 0), 0)),
                  pl.BlockSpec((1, 8, RWKV_PAD),
                               lambda b, i: (b, jnp.minimum((i + 1) * nh8, T // 8 - 1), 0))]
                 + [_full_spec(p.shape) for p in params],
        out_specs=[tok_spec, tok_spec, tok_spec, tok_spec,
                   pl.BlockSpec((2, 1, nc, 1, RWKV_W), lambda b, i: (0, b, i, 0, 0)),
                   one_spec, one_spec, one_spec],
        out_shape=[tok(BF16), tok(BF16), tok(BF16), tok(BF16),
                   jax.ShapeDtypeStruct((2, B, T // CHUNK, 1, RWKV_W), F32),
                   one(BF16), one(BF16), one(BF16)],
        scratch_shapes=[pltpu.VMEM((tm + 16, RWKV_PAD), F32)],
        compiler_params=_cparams(("parallel", "parallel"), V7X_VMEM_LIMIT),
        name="rwkv_prep",
    )(ur, ur, ur, *params)


def _pair_masks():
    lane = lax.broadcasted_iota(jnp.int32, (CHUNK, PAIR), 1)
    row = lax.broadcasted_iota(jnp.int32, (CHUNK, PAIR), 0)
    return lane < HEAD, lane & (HEAD - 1), row


def _bd(x, m0):
    z = jnp.zeros_like(x)
    return jnp.concatenate([jnp.where(m0, x, z), jnp.where(m0, z, x)], axis=0)


def _pair_chunk(Q, P, K, R, V, gc, rev):
    m0, jj, row = _pair_masks()
    m0w = jnp.concatenate([m0, m0], axis=1)
    strict = (jj > row) if rev else (jj < row)
    incl = (jj >= row) if rev else (jj <= row)

    QR = jnp.concatenate([Q, R], axis=0)
    Gp = _dot_nt(QR, _bd(P, m0))
    Gk = _dot_nt(QR, _bd(K, m0))
    Aqp = jnp.where(strict, Gp[:CHUNK], 0.0)
    Arp = jnp.where(incl, Gp[CHUNK:], 0.0).astype(BF16)
    Aqk = jnp.where(strict, Gk[:CHUNK], 0.0)
    Ark = jnp.where(incl, Gk[CHUNK:], 0.0)
    AV = _dot(jnp.concatenate([Aqk, Ark], axis=0).astype(BF16), _bd(V, m0))

    X = jnp.concatenate([Q.astype(F32), AV[:CHUNK]], axis=1)
    pows = [Aqp.astype(BF16)]
    n = 1
    while 2 * n < CHUNK:
        pows.append(_dot(pows[-1], _bd(pows[-1], m0)).astype(BF16))
        n *= 2
    for Pw in reversed(pows):
        X = X + _dot(Pw, _bd(X.astype(BF16), m0w))
    Xb = X.astype(BF16)

    RY = _dot(Arp, _bd(Xb, m0w))
    Rh = R.astype(F32) + RY[:, :PAIR]
    Yi = RY[:, PAIR:] + AV[CHUNK:]
    Ph = (P.astype(F32) * gc).astype(BF16)
    Kh = (K.astype(F32) * gc).astype(BF16)
    MN = _dot_tn(Xb, Ph)
    N2 = _dot_tn(V, Kh)
    Mt = jnp.where(m0, MN[0:HEAD], MN[HEAD:PAIR])
    Nf = MN[PAIR:] + N2
    Nt = jnp.where(m0, Nf[0:HEAD], Nf[HEAD:PAIR])
    return Rh, Yi, Mt, Nt


def _wkv_chunk_kernel(q_ref, p_ref, k_ref, r_ref, gc_ref, v_ref,
                      rh_ref, yi_ref, mt_ref, nt_ref, *, ncb):
    for d in range(2):
        for c in range(ncb):
            sl = slice(c * CHUNK, (c + 1) * CHUNK)
            for pr in range(RWKV_W // PAIR):
                ln = slice(pr * PAIR, (pr + 1) * PAIR)
                Rh, Yi, Mt, Nt = _pair_chunk(q_ref[d, 0, sl, ln], p_ref[d, 0, sl, ln],
                                             k_ref[d, 0, sl, ln], r_ref[d, 0, sl, ln],
                                             v_ref[0, sl, ln], gc_ref[d, 0, c, :, ln], d == 1)
                rh_ref[d, 0, sl, ln] = Rh.astype(BF16)
                yi_ref[d, 0, sl, ln] = Yi.astype(BF16)
                mt_ref[d, 0, sl, ln] = Mt.astype(BF16)
                nt_ref[d, 0, sl, ln] = Nt


def _wkv_chunk(q, p, k, r, gc, v, ncb):
    _, B, T, _ = q.shape
    tc = ncb * CHUNK
    tok_spec = pl.BlockSpec((2, 1, tc, RWKV_W), lambda b, i: (0, b, i, 0))
    tok = lambda dt: jax.ShapeDtypeStruct((2, B, T, RWKV_W), dt)
    return pl.pallas_call(
        functools.partial(_wkv_chunk_kernel, ncb=ncb),
        grid=(B, T // tc),
        in_specs=[tok_spec, tok_spec, tok_spec, tok_spec,
                  pl.BlockSpec((2, 1, ncb, 1, RWKV_W), lambda b, i: (0, b, i, 0, 0)),
                  pl.BlockSpec((1, tc, RWKV_W), lambda b, i: (b, i, 0))],
        out_specs=[tok_spec, tok_spec, tok_spec, tok_spec],
        out_shape=[tok(BF16), tok(BF16), tok(BF16), tok(F32)],
        compiler_params=_cparams(("parallel", "parallel"), 40 * 1024 * 1024),
        name="wkv_chunk",
    )(q, p, k, r, gc, v)


def _wkv_scan_kernel(s0_ref, rhf_ref, yif_ref, mtf_ref, ntf_ref, gcf_ref,
                     rhb_ref, yib_ref, mtb_ref, ntb_ref, gcb_ref,
                     yf_ref, yb_ref, s_ref, *, nb):
    @pl.when(pl.program_id(0) == 0)
    def _():
        s_ref[...] = s0_ref[...]

    m0, _, _ = _pair_masks()
    dirs = ((rhf_ref, yif_ref, mtf_ref, ntf_ref, gcf_ref, yf_ref),
            (rhb_ref, yib_ref, mtb_ref, ntb_ref, gcb_ref, yb_ref))

    def body(b, carry):
        for d, (rh, yi, mt, nt, gc, y) in enumerate(dirs):
            for pr in range(RWKV_W // PAIR):
                ln = slice(pr * PAIR, (pr + 1) * PAIR)
                S = s_ref[d, b, :, ln]
                Sb = S.astype(BF16)
                y[b, :, ln] = _dot_nt(rh[0, b, :, ln], _bd(Sb, m0)) + yi[0, b, :, ln].astype(F32)
                s_ref[d, b, :, ln] = (S * gc[0, b, 0, :, ln] + _dot(Sb, _bd(mt[0, b, :, ln], m0))
                                      + nt[0, b, :, ln])
        return carry

    lax.fori_loop(0, nb, body, 0)


def _wkv_scan(s0, rh, yi, mt, nt, gc):
    _, B, T, _ = rh.shape
    n = T // CHUNK
    f_spec = pl.BlockSpec((1, B, CHUNK, RWKV_W), lambda j: (0, 0, j, 0))
    b_spec = pl.BlockSpec((1, B, CHUNK, RWKV_W), lambda j: (1, 0, n - 1 - j, 0))
    gf_spec = pl.BlockSpec((1, B, 1, 1, RWKV_W), lambda j: (0, 0, j, 0, 0))
    gb_spec = pl.BlockSpec((1, B, 1, 1, RWKV_W), lambda j: (1, 0, n - 1 - j, 0, 0))
    s_spec = pl.BlockSpec((2, B, HEAD, RWKV_W), lambda j: (0, 0, 0, 0))
    return pl.pallas_call(
        functools.partial(_wkv_scan_kernel, nb=B),
        grid=(n,),
        in_specs=[s_spec, f_spec, f_spec, f_spec, f_spec, gf_spec,
                  b_spec, b_spec, b_spec, b_spec, gb_spec],
        out_specs=[pl.BlockSpec((B, CHUNK, RWKV_W), lambda j: (0, j, 0)),
                   pl.BlockSpec((B, CHUNK, RWKV_W), lambda j: (0, n - 1 - j, 0)),
                   s_spec],
        out_shape=[jax.ShapeDtypeStruct((B, T, RWKV_W), F32),
                   jax.ShapeDtypeStruct((B, T, RWKV_W), F32),
                   jax.ShapeDtypeStruct((2, B, HEAD, RWKV_W), F32)],
        compiler_params=_cparams(("arbitrary",), 48 * 1024 * 1024),
        name="wkv_scan",
    )(s0, rh, yi, mt, nt, gc, rh, yi, mt, nt, gc)


CONV_HALO = 16
CONV_RB = 64


def _conv_kernel(cur_ref, prev_ref, next_ref, w_ref, b_ref, ng_ref, nb_ref, bd_ref,
                 o_ref, z_ref, cv_ref, *, tm):
    i = pl.program_id(1)
    nt = pl.num_programs(1)

    def glu(u):
        return u[:, :CONV_W].astype(F32) * _sigmoid(u[:, CONV_W:].astype(F32))

    z_ref[0:CONV_HALO, :] = jnp.where(i > 0, glu(prev_ref[0]), 0.0)
    z_ref[CONV_HALO:CONV_HALO + tm, :] = glu(cur_ref[0])
    z_ref[CONV_HALO + tm:, :] = jnp.where(i < nt - 1, glu(next_ref[0]), 0.0)

    off = CONV_HALO - CONV_K // 2
    for rb in range(tm // CONV_RB):
        for cb in range(CONV_W // 128):
            ln = slice(cb * 128, (cb + 1) * 128)
            acc = jnp.zeros((CONV_RB, 128), F32)
            for j in range(CONV_K):
                r0 = rb * CONV_RB + j + off
                acc = acc + w_ref[j:j + 1, ln] * z_ref[r0:r0 + CONV_RB, ln]
            cv_ref[rb * CONV_RB:(rb + 1) * CONV_RB, ln] = acc

    bd = bd_ref[...]
    y = cv_ref[...] + b_ref[...]
    mu = _dot2(y, bd) * (1.0 / HEAD)
    dlt = y - mu
    var = _dot2(dlt * dlt, bd) * (1.0 / HEAD)
    yn = dlt * lax.rsqrt(var + LN_EPS) * ng_ref[...] + nb_ref[...]
    o_ref[0] = (yn * _sigmoid(yn)).astype(BF16)


def _conv(uc, w, b, ng, nb, bd, tm):
    B, T, _ = uc.shape
    nh = tm // CONV_HALO
    return pl.pallas_call(
        functools.partial(_conv_kernel, tm=tm),
        grid=(B, T // tm),
        in_specs=[pl.BlockSpec((1, tm, 2 * CONV_W), lambda b, i: (b, i, 0)),
                  pl.BlockSpec((1, CONV_HALO, 2 * CONV_W),
                               lambda b, i: (b, jnp.maximum(i * nh - 1, 0), 0)),
                  pl.BlockSpec((1, CONV_HALO, 2 * CONV_W),
                               lambda b, i: (b, jnp.minimum((i + 1) * nh, T // CONV_HALO - 1), 0)),
                  _full_spec(w.shape), _row_spec(CONV_W), _row_spec(CONV_W), _row_spec(CONV_W),
                  _full_spec(bd.shape)],
        out_specs=pl.BlockSpec((1, tm, CONV_W), lambda b, i: (b, i, 0)),
        out_shape=jax.ShapeDtypeStruct((B, T, CONV_W), BF16),
        scratch_shapes=[pltpu.VMEM((tm + 2 * CONV_HALO, CONV_W), F32),
                        pltpu.VMEM((tm, CONV_W), F32)],
        compiler_params=_cparams(("parallel", "parallel"), 40 * 1024 * 1024),
        name="conv",
    )(uc, uc, uc, w, b, ng, nb, bd)


def _out_proj_kernel(x_ref, co_ref, yf_ref, yb_ref, bv_ref, g_ref, ng_ref, nb_ref, bd_ref,
                     w_ref, gt_ref, o_ref):
    bd = bd_ref[...]
    y = yf_ref[0] + yb_ref[0]
    mu = _dot2(y, bd) * (1.0 / HEAD)
    dlt = y - mu
    var = _dot2(dlt * dlt, bd) * (1.0 / HEAD)
    yn = dlt * lax.rsqrt(var + GN_EPS) * ng_ref[...] + nb_ref[...]
    rw = ((yn + bv_ref[0].astype(F32)) * g_ref[0].astype(F32)).astype(BF16)
    res = _dot(co_ref[0], w_ref[0:CONV_W, :]) + _dot(rw, w_ref[CONV_W:, :])
    o_ref[0] = x_ref[0] + gt_ref[0] * res


def _out_proj(x, co, yf, yb, bv, g, ng, nb, bd, w, gt, tm):
    B, T, D = x.shape
    half = pl.BlockSpec((1, tm, RWKV_W), lambda b, i: (b, i, 0))
    full = pl.BlockSpec((1, tm, D), lambda b, i: (b, i, 0))
    return pl.pallas_call(
        _out_proj_kernel,
        grid=(B, T // tm),
        in_specs=[full, half, half, half, half, half, _row_spec(RWKV_W), _row_spec(RWKV_W),
                  _full_spec(bd.shape), _full_spec(w.shape), _mod_spec(gt)],
        out_specs=full,
        out_shape=jax.ShapeDtypeStruct((B, T, D), F32),
        compiler_params=_cparams(("parallel", "parallel"), 48 * 1024 * 1024),
        name="out_proj",
    )(x, co, yf, yb, bv, g, ng, nb, bd, w, gt)


def _ffn_kernel(x_ref, g_ref, sh_ref, sc_ref, gt_ref, wg_ref, wu_ref, wd_ref, o_ref):
    x = x_ref[0]
    h = _rms_mod(x, g_ref[...], sh_ref[0], sc_ref[0]).astype(BF16)
    acc = jnp.zeros(x.shape, F32)
    for j in range(D_FF // FF_CHUNK):
        sl = slice(j * FF_CHUNK, (j + 1) * FF_CHUNK)
        gg = _dot(h, wg_ref[:, sl])
        uu = _dot(h, wu_ref[:, sl])
        acc = acc + _dot((gg * _sigmoid(gg) * uu).astype(BF16), wd_ref[sl, :])
    o_ref[0] = x + gt_ref[0] * acc


def _ffn(x, g, sh, sc, gt, wg, wu, wd, tm):
    B, T, D = x.shape
    full = pl.BlockSpec((1, tm, D), lambda b, i: (b, i, 0))
    return pl.pallas_call(
        _ffn_kernel,
        grid=(B, T // tm),
        in_specs=[full, _row_spec(D), _mod_spec(sh), _mod_spec(sc), _mod_spec(gt),
                  _full_spec(wg.shape), _full_spec(wu.shape), _full_spec(wd.shape)],
        out_specs=full,
        out_shape=jax.ShapeDtypeStruct((B, T, D), F32),
        compiler_params=_cparams(("parallel", "parallel"), V7X_VMEM_LIMIT),
        name="ffn",
    )(x, g, sh, sc, gt, wg, wu, wd)


def _final_norm_kernel(x_ref, g_ref, o_ref):
    x = x_ref[0]
    o_ref[0] = x * lax.rsqrt(jnp.mean(x * x, axis=-1, keepdims=True) + RMS_EPS) * g_ref[...]


def _final_norm(x, g, tm):
    B, T, D = x.shape
    full = pl.BlockSpec((1, tm, D), lambda b, i: (b, i, 0))
    return pl.pallas_call(
        _final_norm_kernel,
        grid=(B, T // tm),
        in_specs=[full, _row_spec(D)],
        out_specs=full,
        out_shape=jax.ShapeDtypeStruct((B, T, D), F32),
        compiler_params=_cparams(("parallel", "parallel")),
        name="final_norm",
    )(x, g)


def _grid_swap(t, rows, cols):
    b, _, ch = t.shape
    return t.reshape(b, rows, cols, ch).transpose(0, 2, 1, 3).reshape(b, rows * cols, ch)


def _tile(T, want):
    t = min(T, want)
    assert T % t == 0 and t % CHUNK == 0, (T, t)
    return t


def _constants():
    idx = jnp.arange(RWKV_W)
    bd = (idx[:, None] // HEAD == idx[None, :] // HEAD).astype(BF16)
    t = jnp.arange(CHUNK)
    tri_f = (t[None, :] <= t[:, None]).astype(BF16)
    tri_b = (t[None, :] >= t[:, None]).astype(BF16)
    return bd, jnp.stack([tri_f, tri_b])


def _layer_params(l, a):
    (w_in, conv_w, conv_b, cnorm_g, cnorm_b, shift_w, decay_b0, decay_up, iclr_b0, iclr_up,
     k_k, k_a, g_up, r_k, gn_g, gn_b, w_out, ffn_wg, ffn_wu, ffn_wd) = a
    pad_c = RWKV_PAD - RWKV_COLS
    row = lambda t: t[l].reshape(1, -1)
    z64 = jnp.zeros((2, 64, RWKV_W), F32)
    return dict(
        wc=w_in[l][:, :2 * CONV_W].astype(BF16),
        wr=jnp.pad(w_in[l][:, 2 * CONV_W:], ((0, 0), (0, pad_c))).astype(BF16),
        shw=jnp.pad(shift_w[l], ((0, 0), (0, pad_c))),
        db0=decay_b0[l].reshape(2, 1, RWKV_W),
        dup=jnp.concatenate([decay_up[l], z64], axis=1).astype(BF16),
        ib0=iclr_b0[l].reshape(2, 1, RWKV_W),
        iup=jnp.concatenate([z64, iclr_up[l]], axis=1).astype(BF16),
        kk=row(k_k), ka=row(k_a), rk=row(r_k),
        gup=jnp.pad(g_up[l], ((0, GATE_PAD - GATE_RANK), (0, 0))).astype(BF16),
        conv_w=conv_w[l], conv_b=row(conv_b), cn_g=row(cnorm_g), cn_b=row(cnorm_b),
        gn_g=row(gn_g), gn_b=row(gn_b),
        w_out=w_out[l].astype(BF16),
        wg=ffn_wg[l].astype(BF16), wu=ffn_wu[l].astype(BF16), wd=ffn_wd[l].astype(BF16),
    )


def _mixer_front(t, g1, sh, sc, pp):
    T = t.shape[1]
    uc, ur = _proj_in(t, g1, sh, sc, pp["wc"], pp["wr"], _tile(T, 512))
    q, p, k, r, gc, v, bv, g = _rwkv_prep(ur, pp, _tile(T, 512))
    rh, yi, mt, nt = _wkv_chunk(q, p, k, r, gc, v, _tile(T, 128) // CHUNK)
    return uc, (rh, yi, mt, nt, gc), (bv, g)


def _mixer_back(t, uc, yf, yb, post, pp, mod, g2):
    sh1, sc1, gt1, sh2, sc2, gt2 = mod
    T = t.shape[1]
    bv, g = post
    co = _conv(uc, pp["conv_w"], pp["conv_b"], pp["cn_g"], pp["cn_b"], pp["bd"], _tile(T, 256))
    t = _out_proj(t, co, yf, yb, bv, g, pp["gn_g"], pp["gn_b"], pp["bd"], pp["w_out"], gt1,
                  _tile(T, 512))
    return _ffn(t, g2, sh2, sc2, gt2, pp["wg"], pp["wu"], pp["wd"], _tile(T, 512))


def kernel(x, c, ctx, c_ctx, ada_w, ada_b, norm1_g, norm2_g, w_in, conv_w, conv_b, cnorm_g, cnorm_b, shift_w, decay_b0, decay_up, iclr_b0, iclr_up, k_k, k_a, g_up, r_k, gn_g, gn_b, w_out, ffn_wg, ffn_wu, ffn_wd, final_g):
    B, S, D = x.shape
    L = ada_w.shape[0]
    rows = S // GRID_W
    assert D == D_MODEL and S % GRID_W == 0

    n_rows = -(-(B + 1) // 8) * 8
    cc = jnp.concatenate([c, c_ctx[None, :], jnp.zeros((n_rows - B - 1, D), F32)], axis=0)
    mods = _ada(cc, ada_w, ada_b)

    bd, tri = _constants()
    layer_args = (w_in, conv_w, conv_b, cnorm_g, cnorm_b, shift_w, decay_b0, decay_up, iclr_b0,
                  iclr_up, k_k, k_a, g_up, r_k, gn_g, gn_b, w_out, ffn_wg, ffn_wu, ffn_wd)
    s_zero = jnp.zeros((2, B, HEAD, RWKV_W), F32)
    col_order = False

    for l in range(L):
        last = l == L - 1
        pp = _layer_params(l, layer_args)
        pp["bd"], pp["tri"] = bd, tri
        mod_x = tuple(mods[l, :B, i * D:(i + 1) * D].reshape(B, 1, D) for i in range(6))
        mod_c = tuple(mods[l, B:B + 1, i * D:(i + 1) * D].reshape(1, 1, D) for i in range(6))
        g1 = norm1_g[l].reshape(1, D)
        g2 = norm2_g[l].reshape(1, D)

        if (l % 2 == 1) != col_order:
            x = _grid_swap(x, GRID_W, rows) if col_order else _grid_swap(x, rows, GRID_W)
            col_order = not col_order

        uc_c, trans_c, post_c = _mixer_front(ctx, g1, mod_c[0], mod_c[1], pp)
        uc_x, trans_x, post_x = _mixer_front(x, g1, mod_x[0], mod_x[1], pp)
        yc_f, yc_b, s_ctx = _wkv_scan(s_zero, *trans_c)
        yx_f, yx_b, _ = _wkv_scan(s_ctx, *trans_x)

        x = _mixer_back(x, uc_x, yx_f, yx_b, post_x, pp, mod_x, g2)
        if not last:
            ctx = _mixer_back(ctx, uc_c, yc_f, yc_b, post_c, pp, mod_c, g2)

    out = _final_norm(x, final_g.reshape(1, D), _tile(S, 512))
    if col_order:
        out = _grid_swap(out, GRID_W, rows)
    return out
```

```python
import functools

import jax
import jax.numpy as jnp
from jax import lax
from jax.experimental import pallas as pl
from jax.experimental.pallas import tpu as pltpu

F32 = jnp.float32
BF16 = jnp.bfloat16

D_MODEL = 1024
CONV_W = 512
CONV_K = 31
RWKV_W = 512
HEAD = 64
DECAY_RANK = 64
ICLR_RANK = 64
GATE_RANK = 160
RWKV_COLS = 3 * RWKV_W + DECAY_RANK + ICLR_RANK + GATE_RANK
RWKV_PAD = 1920
LORA_OFF = 3 * RWKV_W
GATE_OFF = LORA_OFF + 128
GATE_PAD = RWKV_PAD - GATE_OFF
D_FF = 2816
FF_CHUNK = 256
GRID_W = 64
RMS_EPS = 1e-6
LN_EPS = 1e-5
GN_EPS = 64e-5
DECAY_SCALE = 0.606531
CHUNK = 64
PAIR = 2 * HEAD
V7X_VMEM_LIMIT = 56 * 1024 * 1024


def _cparams(sem, vmem_bytes=None):
    return pltpu.CompilerParams(dimension_semantics=sem, vmem_limit_bytes=vmem_bytes)


def _dot(a, b):
    return jnp.dot(a, b, preferred_element_type=F32)


def _dot_nt(a, b):
    return lax.dot_general(a, b, (((1,), (1,)), ((), ())), preferred_element_type=F32)


def _dot_tn(a, b):
    return lax.dot_general(a, b, (((0,), (0,)), ((), ())), preferred_element_type=F32)


def _sigmoid(x):
    return 1.0 / (1.0 + jnp.exp(-x))


def _dot2(x, m):
    hi = x.astype(BF16)
    lo = (x - hi.astype(F32)).astype(BF16)
    return _dot(hi, m) + _dot(lo, m)


def _rms_mod(x, g, shift, scale):
    y = x * lax.rsqrt(jnp.mean(x * x, axis=-1, keepdims=True) + RMS_EPS)
    return (y * g) * (1.0 + scale) + shift


def _ada_kernel(cc_ref, w_ref, b_ref, o_ref):
    s = cc_ref[...]
    a = (s * _sigmoid(s)).astype(BF16)
    o_ref[0] = _dot(a, w_ref[0].astype(BF16)) + b_ref[0]


def _ada(cc, ada_w, ada_b):
    L, D, N = ada_w.shape
    R = cc.shape[0]
    tn = 1536
    return pl.pallas_call(
        _ada_kernel,
        grid=(L, N // tn),
        in_specs=[pl.BlockSpec((R, D), lambda l, j: (0, 0)),
                  pl.BlockSpec((1, D, tn), lambda l, j: (l, 0, j)),
                  pl.BlockSpec((1, 1, tn), lambda l, j: (l, 0, j))],
        out_specs=pl.BlockSpec((1, R, tn), lambda l, j: (l, 0, j)),
        out_shape=jax.ShapeDtypeStruct((L, R, N), F32),
        compiler_params=_cparams(("arbitrary", "arbitrary"), 40 * 1024 * 1024),
        name="ada",
    )(cc, ada_w, ada_b.reshape(L, 1, N))


def _mod_spec(m):
    if m.shape[0] == 1:
        return pl.BlockSpec((1, 1, D_MODEL), lambda b, i: (0, 0, 0))
    return pl.BlockSpec((1, 1, D_MODEL), lambda b, i: (b, 0, 0))


def _row_spec(n):
    return pl.BlockSpec((1, n), lambda b, i: (0, 0))


def _full_spec(shape):
    nd = len(shape)
    return pl.BlockSpec(shape, lambda b, i: (0,) * nd)


def _proj_in_kernel(x_ref, g_ref, sh_ref, sc_ref, wc_ref, wr_ref, uc_ref, ur_ref):
    h = _rms_mod(x_ref[0], g_ref[...], sh_ref[0], sc_ref[0]).astype(BF16)
    uc_ref[0] = _dot(h, wc_ref[...]).astype(BF16)
    ur_ref[0] = _dot(h, wr_ref[...])


def _proj_in(x, g, sh, sc, wc, wr, tm):
    B, T, D = x.shape
    return pl.pallas_call(
        _proj_in_kernel,
        grid=(B, T // tm),
        in_specs=[pl.BlockSpec((1, tm, D), lambda b, i: (b, i, 0)),
                  _row_spec(D), _mod_spec(sh), _mod_spec(sc),
                  _full_spec(wc.shape), _full_spec(wr.shape)],
        out_specs=[pl.BlockSpec((1, tm, 2 * CONV_W), lambda b, i: (b, i, 0)),
                   pl.BlockSpec((1, tm, RWKV_PAD), lambda b, i: (b, i, 0))],
        out_shape=[jax.ShapeDtypeStruct((B, T, 2 * CONV_W), BF16),
                   jax.ShapeDtypeStruct((B, T, RWKV_PAD), F32)],
        compiler_params=_cparams(("parallel", "parallel"), 48 * 1024 * 1024),
        name="proj_in",
    )(x, g, sh, sc, wc, wr)


def _rwkv_prep_kernel(cur_ref, prev_ref, next_ref, shw_ref, db0_ref, dup_ref, ib0_ref, iup_ref,
                      kk_ref, ka_ref, gup_ref, rk_ref, bd_ref, tri_ref,
                      q_ref, p_ref, k_ref, r_ref, gc_ref, v_ref, bv_ref, g_ref, *, tm):
    i = pl.program_id(1)
    nt = pl.num_programs(1)

    def shifted(c0, c1):
        u0 = cur_ref[0, :, c0:c1]
        row8 = lax.broadcasted_iota(jnp.int32, (8, c1 - c0), 0)
        before = jnp.where(i > 0, prev_ref[0, 7:8, c0:c1], 0.0)
        after = jnp.where(i < nt - 1, next_ref[0, 0:1, c0:c1], 0.0)
        up = pltpu.roll(u0, 1, 0)
        up = jnp.concatenate([jnp.where(row8 == 0, before, up[0:8]), up[8:]], axis=0)
        un = pltpu.roll(u0, tm - 1, 0)
        un = jnp.concatenate([un[:tm - 8], jnp.where(row8 == 7, after, un[tm - 8:])], axis=0)
        return shw_ref[0:1, c0:c1] * up + shw_ref[1:2, c0:c1] * u0 + shw_ref[2:3, c0:c1] * un

    r = shifted(0, RWKV_W)
    k = shifted(RWKV_W, 2 * RWKV_W)
    v = shifted(2 * RWKV_W, 3 * RWKV_W)
    wa = shifted(LORA_OFF, LORA_OFF + 128)
    gl = shifted(GATE_OFF, RWKV_PAD)
    bd = bd_ref[...]

    kkr = k * kk_ref[...]
    kk = kkr / jnp.maximum(jnp.sqrt(_dot2(kkr * kkr, bd)), 1e-12)
    lane = lax.broadcasted_iota(jnp.int32, (tm, 128), 1)
    wa_t = jnp.where(lane < DECAY_RANK, jnp.tanh(wa), wa).astype(BF16)

    v_ref[0] = v.astype(BF16)
    g_ref[0] = _dot(_sigmoid(gl).astype(BF16), gup_ref[...]).astype(BF16)
    bv_ref[0] = (_dot2(r * k * rk_ref[...], bd) * v).astype(BF16)

    for d in range(2):
        lw = -DECAY_SCALE * _sigmoid(db0_ref[d] + _dot(wa_t, dup_ref[d]))
        a = _sigmoid(ib0_ref[d] + _dot(wa_t, iup_ref[d]))
        kd = k * (1.0 + (a - 1.0) * ka_ref[...])
        b = kk * a
        tri = tri_ref[d]
        for c in range(tm // CHUNK):
            sl = slice(c * CHUNK, (c + 1) * CHUNK)
            lwc = lw[sl]
            cum = _dot2_lhs(tri, lwc)
            tot = cum[0:1] if d == 1 else cum[CHUNK - 1:CHUNK]
            e_neg = jnp.exp(-cum)
            q_ref[d, 0, sl, :] = (-kk[sl] * jnp.exp(cum - lwc)).astype(BF16)
            p_ref[d, 0, sl, :] = (b[sl] * e_neg).astype(BF16)
            k_ref[d, 0, sl, :] = (kd[sl] * e_neg).astype(BF16)
            r_ref[d, 0, sl, :] = (r[sl] * jnp.exp(cum)).astype(BF16)
            gc_ref[d, 0, c] = jnp.exp(tot)


def _dot2_lhs(m, x):
    hi = x.astype(BF16)
    lo = (x - hi.astype(F32)).astype(BF16)
    return _dot(m, hi) + _dot(m, lo)


def _rwkv_prep(ur, pp, tm):
    B, T, _ = ur.shape
    nh8 = tm // 8
    nc = tm // CHUNK
    tok = lambda dt: jax.ShapeDtypeStruct((2, B, T, RWKV_W), dt)
    tok_spec = pl.BlockSpec((2, 1, tm, RWKV_W), lambda b, i: (0, b, i, 0))
    one = lambda dt: jax.ShapeDtypeStruct((B, T, RWKV_W), dt)
    one_spec = pl.BlockSpec((1, tm, RWKV_W), lambda b, i: (b, i, 0))
    params = (pp["shw"], pp["db0"], pp["dup"], pp["ib0"], pp["iup"], pp["kk"], pp["ka"],
              pp["gup"], pp["rk"], pp["bd"], pp["tri"])
    return pl.pallas_call(
        functools.partial(_rwkv_prep_kernel, tm=tm),
        grid=(B, T // tm),
        in_specs=[pl.BlockSpec((1, tm, RWKV_PAD), lambda b, i: (b, i, 0)),
                  pl.BlockSpec((1, 8, RWKV_PAD), lambda b, i: (b, jnp.maximum(i * nh8 - 1, 0), 0)),
                  pl.BlockSpec((1, 8, RWKV_PAD),
                               lambda b, i: (b, jnp.minimum((i + 1) * nh8, T // 8 - 1), 0))]
                 + [_full_spec(p.shape) for p in params],
        out_specs=[tok_spec, tok_spec, tok_spec, tok_spec,
                   pl.BlockSpec((2, 1, nc, 1, RWKV_W), lambda b, i: (0, b, i, 0, 0)),
                   one_spec, one_spec, one_spec],
        out_shape=[tok(BF16), tok(BF16), tok(BF16), tok(BF16),
                   jax.ShapeDtypeStruct((2, B, T // CHUNK, 1, RWKV_W), F32),
                   one(BF16), one(BF16), one(BF16)],
        compiler_params=_cparams(("parallel", "parallel"), V7X_VMEM_LIMIT),
        name="rwkv_prep",
    )(ur, ur, ur, *params)


def _pair_masks():
    lane = lax.broadcasted_iota(jnp.int32, (CHUNK, PAIR), 1)
    row = lax.broadcasted_iota(jnp.int32, (CHUNK, PAIR), 0)
    return lane < HEAD, lane & (HEAD - 1), row


def _bd(x, m0):
    z = jnp.zeros_like(x)
    return jnp.concatenate([jnp.where(m0, x, z), jnp.where(m0, z, x)], axis=0)


def _pair_chunks(items):
    m0, jj, row = _pair_masks()
    m0w = jnp.concatenate([m0, m0], axis=1)
    masks = {False: (jj < row, jj <= row), True: (jj > row, jj >= row)}
    n = len(items)
    Q, P, K, R, V, gc, rev = zip(*items)
    strict = [masks[v][0] for v in rev]
    incl = [masks[v][1] for v in rev]

    QR = [jnp.concatenate([Q[i], R[i]], axis=0) for i in range(n)]
    Gp = [_dot_nt(QR[i], _bd(P[i], m0)) for i in range(n)]
    Gk = [_dot_nt(QR[i], _bd(K[i], m0)) for i in range(n)]
    Arp = [jnp.where(incl[i], Gp[i][CHUNK:], 0.0).astype(BF16) for i in range(n)]
    Aqk = [jnp.where(strict[i], Gk[i][:CHUNK], 0.0) for i in range(n)]
    Ark = [jnp.where(incl[i], Gk[i][CHUNK:], 0.0) for i in range(n)]
    AV = [_dot(jnp.concatenate([Aqk[i], Ark[i]], axis=0).astype(BF16), _bd(V[i], m0))
          for i in range(n)]

    X = [jnp.concatenate([Q[i].astype(F32), AV[i][:CHUNK]], axis=1) for i in range(n)]
    pows = [[jnp.where(strict[i], Gp[i][:CHUNK], 0.0).astype(BF16) for i in range(n)]]
    span = 1
    while 2 * span < CHUNK:
        pows.append([_dot(pw, _bd(pw, m0)).astype(BF16) for pw in pows[-1]])
        span *= 2
    for level in reversed(pows):
        X = [X[i] + _dot(level[i], _bd(X[i].astype(BF16), m0w)) for i in range(n)]
    Xb = [x.astype(BF16) for x in X]

    RY = [_dot(Arp[i], _bd(Xb[i], m0w)) for i in range(n)]
    Ph = [(P[i].astype(F32) * gc[i]).astype(BF16) for i in range(n)]
    Kh = [(K[i].astype(F32) * gc[i]).astype(BF16) for i in range(n)]
    MN = [_dot_tn(Xb[i], Ph[i]) for i in range(n)]
    N2 = [_dot_tn(V[i], Kh[i]) for i in range(n)]
    out = []
    for i in range(n):
        Rh = R[i].astype(F32) + RY[i][:, :PAIR]
        Yi = RY[i][:, PAIR:] + AV[i][CHUNK:]
        Mt = jnp.where(m0, MN[i][0:HEAD], MN[i][HEAD:PAIR])
        Nf = MN[i][PAIR:] + N2[i]
        Nt = jnp.where(m0, Nf[0:HEAD], Nf[HEAD:PAIR])
        out.append((Rh, Yi, Mt, Nt))
    return out


def _wkv_chunk_kernel(q_ref, p_ref, k_ref, r_ref, gc_ref, v_ref,
                      rh_ref, yi_ref, mt_ref, nt_ref, *, ncb):
    where = [(d, c, pr) for d in range(2) for c in range(ncb) for pr in range(RWKV_W // PAIR)]
    items = []
    for d, c, pr in where:
        sl = slice(c * CHUNK, (c + 1) * CHUNK)
        ln = slice(pr * PAIR, (pr + 1) * PAIR)
        items.append((q_ref[d, 0, sl, ln], p_ref[d, 0, sl, ln], k_ref[d, 0, sl, ln],
                      r_ref[d, 0, sl, ln], v_ref[0, sl, ln], gc_ref[d, 0, c, :, ln], d == 1))
    for (d, c, pr), (Rh, Yi, Mt, Nt) in zip(where, _pair_chunks(items)):
        sl = slice(c * CHUNK, (c + 1) * CHUNK)
        ln = slice(pr * PAIR, (pr + 1) * PAIR)
        rh_ref[d, 0, sl, ln] = Rh.astype(BF16)
        yi_ref[d, 0, sl, ln] = Yi.astype(BF16)
        mt_ref[d, 0, sl, ln] = Mt.astype(BF16)
        nt_ref[d, 0, sl, ln] = Nt


def _wkv_chunk(q, p, k, r, gc, v, ncb):
    _, B, T, _ = q.shape
    tc = ncb * CHUNK
    tok_spec = pl.BlockSpec((2, 1, tc, RWKV_W), lambda b, i: (0, b, i, 0))
    tok = lambda dt: jax.ShapeDtypeStruct((2, B, T, RWKV_W), dt)
    return pl.pallas_call(
        functools.partial(_wkv_chunk_kernel, ncb=ncb),
        grid=(B, T // tc),
        in_specs=[tok_spec, tok_spec, tok_spec, tok_spec,
                  pl.BlockSpec((2, 1, ncb, 1, RWKV_W), lambda b, i: (0, b, i, 0, 0)),
                  pl.BlockSpec((1, tc, RWKV_W), lambda b, i: (b, i, 0))],
        out_specs=[tok_spec, tok_spec, tok_spec, tok_spec],
        out_shape=[tok(BF16), tok(BF16), tok(BF16), tok(F32)],
        compiler_params=_cparams(("parallel", "parallel"), 40 * 1024 * 1024),
        name="wkv_chunk",
    )(q, p, k, r, gc, v)


def _wkv_scan_kernel(s0_ref, rhf_ref, yif_ref, mtf_ref, ntf_ref, gcf_ref,
                     rhb_ref, yib_ref, mtb_ref, ntb_ref, gcb_ref,
                     yf_ref, yb_ref, s_ref, *, nb):
    @pl.when(pl.program_id(0) == 0)
    def _():
        s_ref[...] = s0_ref[...]

    m0, _, _ = _pair_masks()
    dirs = ((rhf_ref, yif_ref, mtf_ref, ntf_ref, gcf_ref, yf_ref),
            (rhb_ref, yib_ref, mtb_ref, ntb_ref, gcb_ref, yb_ref))

    def body(b, carry):
        for d, (rh, yi, mt, nt, gc, y) in enumerate(dirs):
            for pr in range(RWKV_W // PAIR):
                ln = slice(pr * PAIR, (pr + 1) * PAIR)
                S = s_ref[d, b, :, ln]
                Sb = S.astype(BF16)
                y[b, :, ln] = _dot_nt(rh[0, b, :, ln], _bd(Sb, m0)) + yi[0, b, :, ln].astype(F32)
                s_ref[d, b, :, ln] = (S * gc[0, b, 0, :, ln] + _dot(Sb, _bd(mt[0, b, :, ln], m0))
                                      + nt[0, b, :, ln])
        return carry

    lax.fori_loop(0, nb, body, 0)


def _wkv_scan(s0, rh, yi, mt, nt, gc):
    _, B, T, _ = rh.shape
    n = T // CHUNK
    f_spec = pl.BlockSpec((1, B, CHUNK, RWKV_W), lambda j: (0, 0, j, 0))
    b_spec = pl.BlockSpec((1, B, CHUNK, RWKV_W), lambda j: (1, 0, n - 1 - j, 0))
    gf_spec = pl.BlockSpec((1, B, 1, 1, RWKV_W), lambda j: (0, 0, j, 0, 0))
    gb_spec = pl.BlockSpec((1, B, 1, 1, RWKV_W), lambda j: (1, 0, n - 1 - j, 0, 0))
    s_spec = pl.BlockSpec((2, B, HEAD, RWKV_W), lambda j: (0, 0, 0, 0))
    return pl.pallas_call(
        functools.partial(_wkv_scan_kernel, nb=B),
        grid=(n,),
        in_specs=[s_spec, f_spec, f_spec, f_spec, f_spec, gf_spec,
                  b_spec, b_spec, b_spec, b_spec, gb_spec],
        out_specs=[pl.BlockSpec((B, CHUNK, RWKV_W), lambda j: (0, j, 0)),
                   pl.BlockSpec((B, CHUNK, RWKV_W), lambda j: (0, n - 1 - j, 0)),
                   s_spec],
        out_shape=[jax.ShapeDtypeStruct((B, T, RWKV_W), F32),
                   jax.ShapeDtypeStruct((B, T, RWKV_W), F32),
                   jax.ShapeDtypeStruct((2, B, HEAD, RWKV_W), F32)],
        compiler_params=_cparams(("arbitrary",), 48 * 1024 * 1024),
        name="wkv_scan",
    )(s0, rh, yi, mt, nt, gc, rh, yi, mt, nt, gc)


CONV_HALO = 16


def _conv_kernel(cur_ref, prev_ref, next_ref, w_ref, b_ref, ng_ref, nb_ref, bd_ref,
                 o_ref, z_ref, cv_ref, *, tm):
    i = pl.program_id(1)
    nt = pl.num_programs(1)

    def glu(u):
        return u[:, :CONV_W].astype(F32) * _sigmoid(u[:, CONV_W:].astype(F32))

    z_ref[0:CONV_HALO, :] = jnp.where(i > 0, glu(prev_ref[0]), 0.0)
    z_ref[CONV_HALO:CONV_HALO + tm, :] = glu(cur_ref[0])
    z_ref[CONV_HALO + tm:, :] = jnp.where(i < nt - 1, glu(next_ref[0]), 0.0)

    off = CONV_HALO - CONV_K // 2
    rows = tm + 2 * CONV_HALO
    for cb in range(CONV_W // 128):
        ln = slice(cb * 128, (cb + 1) * 128)
        zc = z_ref[:, ln]
        acc = jnp.zeros((tm, 128), F32)
        for s in range(8):
            zs = pltpu.roll(zc, rows - s, 0) if s else zc
            for j in range(CONV_K):
                if (j + off) % 8 == s:
                    a8 = (j + off) - s
                    acc = acc + w_ref[j:j + 1, ln] * zs[a8:a8 + tm]
        cv_ref[:, ln] = acc

    bd = bd_ref[...]
    y = cv_ref[...] + b_ref[...]
    mu = _dot2(y, bd) * (1.0 / HEAD)
    dlt = y - mu
    var = _dot2(dlt * dlt, bd) * (1.0 / HEAD)
    yn = dlt * lax.rsqrt(var + LN_EPS) * ng_ref[...] + nb_ref[...]
    o_ref[0] = (yn * _sigmoid(yn)).astype(BF16)


def _conv(uc, w, b, ng, nb, bd, tm):
    B, T, _ = uc.shape
    nh = tm // CONV_HALO
    return pl.pallas_call(
        functools.partial(_conv_kernel, tm=tm),
        grid=(B, T // tm),
        in_specs=[pl.BlockSpec((1, tm, 2 * CONV_W), lambda b, i: (b, i, 0)),
                  pl.BlockSpec((1, CONV_HALO, 2 * CONV_W),
                               lambda b, i: (b, jnp.maximum(i * nh - 1, 0), 0)),
                  pl.BlockSpec((1, CONV_HALO, 2 * CONV_W),
                               lambda b, i: (b, jnp.minimum((i + 1) * nh, T // CONV_HALO - 1), 0)),
                  _full_spec(w.shape), _row_spec(CONV_W), _row_spec(CONV_W), _row_spec(CONV_W),
                  _full_spec(bd.shape)],
        out_specs=pl.BlockSpec((1, tm, CONV_W), lambda b, i: (b, i, 0)),
        out_shape=jax.ShapeDtypeStruct((B, T, CONV_W), BF16),
        scratch_shapes=[pltpu.VMEM((tm + 2 * CONV_HALO, CONV_W), F32),
                        pltpu.VMEM((tm, CONV_W), F32)],
        compiler_params=_cparams(("parallel", "parallel"), 40 * 1024 * 1024),
        name="conv",
    )(uc, uc, uc, w, b, ng, nb, bd)


def _out_proj_kernel(x_ref, co_ref, yf_ref, yb_ref, bv_ref, g_ref, ng_ref, nb_ref, bd_ref,
                     w_ref, gt_ref, o_ref):
    bd = bd_ref[...]
    y = yf_ref[0] + yb_ref[0]
    mu = _dot2(y, bd) * (1.0 / HEAD)
    dlt = y - mu
    var = _dot2(dlt * dlt, bd) * (1.0 / HEAD)
    yn = dlt * lax.rsqrt(var + GN_EPS) * ng_ref[...] + nb_ref[...]
    rw = ((yn + bv_ref[0].astype(F32)) * g_ref[0].astype(F32)).astype(BF16)
    res = _dot(co_ref[0], w_ref[0:CONV_W, :]) + _dot(rw, w_ref[CONV_W:, :])
    o_ref[0] = x_ref[0] + gt_ref[0] * res


def _out_proj(x, co, yf, yb, bv, g, ng, nb, bd, w, gt, tm):
    B, T, D = x.shape
    half = pl.BlockSpec((1, tm, RWKV_W), lambda b, i: (b, i, 0))
    full = pl.BlockSpec((1, tm, D), lambda b, i: (b, i, 0))
    return pl.pallas_call(
        _out_proj_kernel,
        grid=(B, T // tm),
        in_specs=[full, half, half, half, half, half, _row_spec(RWKV_W), _row_spec(RWKV_W),
                  _full_spec(bd.shape), _full_spec(w.shape), _mod_spec(gt)],
        out_specs=full,
        out_shape=jax.ShapeDtypeStruct((B, T, D), F32),
        compiler_params=_cparams(("parallel", "parallel"), 48 * 1024 * 1024),
        name="out_proj",
    )(x, co, yf, yb, bv, g, ng, nb, bd, w, gt)


def _ffn_kernel(x_ref, g_ref, sh_ref, sc_ref, gt_ref, wg_ref, wu_ref, wd_ref, o_ref):
    x = x_ref[0]
    h = _rms_mod(x, g_ref[...], sh_ref[0], sc_ref[0]).astype(BF16)
    acc = jnp.zeros(x.shape, F32)
    for j in range(D_FF // FF_CHUNK):
        sl = slice(j * FF_CHUNK, (j + 1) * FF_CHUNK)
        gg = _dot(h, wg_ref[:, sl])
        uu = _dot(h, wu_ref[:, sl])
        acc = acc + _dot((gg * _sigmoid(gg) * uu).astype(BF16), wd_ref[sl, :])
    o_ref[0] = x + gt_ref[0] * acc


def _ffn(x, g, sh, sc, gt, wg, wu, wd, tm):
    B, T, D = x.shape
    full = pl.BlockSpec((1, tm, D), lambda b, i: (b, i, 0))
    return pl.pallas_call(
        _ffn_kernel,
        grid=(B, T // tm),
        in_specs=[full, _row_spec(D), _mod_spec(sh), _mod_spec(sc), _mod_spec(gt),
                  _full_spec(wg.shape), _full_spec(wu.shape), _full_spec(wd.shape)],
        out_specs=full,
        out_shape=jax.ShapeDtypeStruct((B, T, D), F32),
        compiler_params=_cparams(("parallel", "parallel"), V7X_VMEM_LIMIT),
        name="ffn",
    )(x, g, sh, sc, gt, wg, wu, wd)


def _final_norm_kernel(x_ref, g_ref, o_ref):
    x = x_ref[0]
    o_ref[0] = x * lax.rsqrt(jnp.mean(x * x, axis=-1, keepdims=True) + RMS_EPS) * g_ref[...]


def _final_norm(x, g, tm):
    B, T, D = x.shape
    full = pl.BlockSpec((1, tm, D), lambda b, i: (b, i, 0))
    return pl.pallas_call(
        _final_norm_kernel,
        grid=(B, T // tm),
        in_specs=[full, _row_spec(D)],
        out_specs=full,
        out_shape=jax.ShapeDtypeStruct((B, T, D), F32),
        compiler_params=_cparams(("parallel", "parallel")),
        name="final_norm",
    )(x, g)


def _grid_swap(t, rows, cols):
    b, _, ch = t.shape
    return t.reshape(b, rows, cols, ch).transpose(0, 2, 1, 3).reshape(b, rows * cols, ch)


def _tile(T, want):
    t = min(T, want)
    assert T % t == 0 and t % CHUNK == 0, (T, t)
    return t


def _constants():
    idx = jnp.arange(RWKV_W)
    bd = (idx[:, None] // HEAD == idx[None, :] // HEAD).astype(BF16)
    t = jnp.arange(CHUNK)
    tri_f = (t[None, :] <= t[:, None]).astype(BF16)
    tri_b = (t[None, :] >= t[:, None]).astype(BF16)
    return bd, jnp.stack([tri_f, tri_b])


def _layer_params(l, a):
    (w_in, conv_w, conv_b, cnorm_g, cnorm_b, shift_w, decay_b0, decay_up, iclr_b0, iclr_up,
     k_k, k_a, g_up, r_k, gn_g, gn_b, w_out, ffn_wg, ffn_wu, ffn_wd) = a
    pad_c = RWKV_PAD - RWKV_COLS
    row = lambda t: t[l].reshape(1, -1)
    z64 = jnp.zeros((2, 64, RWKV_W), F32)
    return dict(
        wc=w_in[l][:, :2 * CONV_W].astype(BF16),
        wr=jnp.pad(w_in[l][:, 2 * CONV_W:], ((0, 0), (0, pad_c))).astype(BF16),
        shw=jnp.pad(shift_w[l], ((0, 0), (0, pad_c))),
        db0=decay_b0[l].reshape(2, 1, RWKV_W),
        dup=jnp.concatenate([decay_up[l], z64], axis=1).astype(BF16),
        ib0=iclr_b0[l].reshape(2, 1, RWKV_W),
        iup=jnp.concatenate([z64, iclr_up[l]], axis=1).astype(BF16),
        kk=row(k_k), ka=row(k_a), rk=row(r_k),
        gup=jnp.pad(g_up[l], ((0, GATE_PAD - GATE_RANK), (0, 0))).astype(BF16),
        conv_w=conv_w[l], conv_b=row(conv_b), cn_g=row(cnorm_g), cn_b=row(cnorm_b),
        gn_g=row(gn_g), gn_b=row(gn_b),
        w_out=w_out[l].astype(BF16),
        wg=ffn_wg[l].astype(BF16), wu=ffn_wu[l].astype(BF16), wd=ffn_wd[l].astype(BF16),
    )


def _mixer_front(t, g1, sh, sc, pp):
    T = t.shape[1]
    uc, ur = _proj_in(t, g1, sh, sc, pp["wc"], pp["wr"], _tile(T, 512))
    q, p, k, r, gc, v, bv, g = _rwkv_prep(ur, pp, _tile(T, 512))
    rh, yi, mt, nt = _wkv_chunk(q, p, k, r, gc, v, _tile(T, 128) // CHUNK)
    return uc, (rh, yi, mt, nt, gc), (bv, g)


def _mixer_back(t, uc, yf, yb, post, pp, mod, g2):
    sh1, sc1, gt1, sh2, sc2, gt2 = mod
    T = t.shape[1]
    bv, g = post
    co = _conv(uc, pp["conv_w"], pp["conv_b"], pp["cn_g"], pp["cn_b"], pp["bd"], _tile(T, 256))
    t = _out_proj(t, co, yf, yb, bv, g, pp["gn_g"], pp["gn_b"], pp["bd"], pp["w_out"], gt1,
                  _tile(T, 512))
    return _ffn(t, g2, sh2, sc2, gt2, pp["wg"], pp["wu"], pp["wd"], _tile(T, 512))


def kernel(x, c, ctx, c_ctx, ada_w, ada_b, norm1_g, norm2_g, w_in, conv_w, conv_b, cnorm_g, cnorm_b, shift_w, decay_b0, decay_up, iclr_b0, iclr_up, k_k, k_a, g_up, r_k, gn_g, gn_b, w_out, ffn_wg, ffn_wu, ffn_wd, final_g):
    B, S, D = x.shape
    L = ada_w.shape[0]
    rows = S // GRID_W
    assert D == D_MODEL and S % GRID_W == 0

    n_rows = -(-(B + 1) // 8) * 8
    cc = jnp.concatenate([c, c_ctx[None, :], jnp.zeros((n_rows - B - 1, D), F32)], axis=0)
    mods = _ada(cc, ada_w, ada_b)

    bd, tri = _constants()
    layer_args = (w_in, conv_w, conv_b, cnorm_g, cnorm_b, shift_w, decay_b0, decay_up, iclr_b0,
                  iclr_up, k_k, k_a, g_up, r_k, gn_g, gn_b, w_out, ffn_wg, ffn_wu, ffn_wd)
    s_zero = jnp.zeros((2, B, HEAD, RWKV_W), F32)
    col_order = False

    for l in range(L):
        last = l == L - 1
        pp = _layer_params(l, layer_args)
        pp["bd"], pp["tri"] = bd, tri
        mod_x = tuple(mods[l, :B, i * D:(i + 1) * D].reshape(B, 1, D) for i in range(6))
        mod_c = tuple(mods[l, B:B + 1, i * D:(i + 1) * D].reshape(1, 1, D) for i in range(6))
        g1 = norm1_g[l].reshape(1, D)
        g2 = norm2_g[l].reshape(1, D)

        if (l % 2 == 1) != col_order:
            x = _grid_swap(x, GRID_W, rows) if col_order else _grid_swap(x, rows, GRID_W)
            col_order = not col_order

        uc_c, trans_c, post_c = _mixer_front(ctx, g1, mod_c[0], mod_c[1], pp)
        uc_x, trans_x, post_x = _mixer_front(x, g1, mod_x[0], mod_x[1], pp)
        yc_f, yc_b, s_ctx = _wkv_scan(s_zero, *trans_c)
        yx_f, yx_b, _ = _wkv_scan(s_ctx, *trans_x)

        x = _mixer_back(x, uc_x, yx_f, yx_b, post_x, pp, mod_x, g2)
        if not last:
            ctx = _mixer_back(ctx, uc_c, yc_f, yc_b, post_c, pp, mod_c, g2)

    out = _final_norm(x, final_g.reshape(1, D), _tile(S, 512))
    if col_order:
        out = _grid_swap(out, GRID_W, rows)
    return out
```
